```python
import math
import jax, jax.numpy as jnp
from jax import lax
import numpy as np

D_MODEL = 2048
BATCH = 8
SEQ = 4096
DEPTH = 2

GRID_W = 64
CTX_LEN = 256
D_POOL = D_MODEL // 2
D_ATTN = D_MODEL // 2
D_MIX = D_POOL + D_ATTN
POOL_WINDOWS = (2, 4, 8, 16)
N_POOL_GROUPS = len(POOL_WINDOWS)
POOL_GROUP = D_POOL // N_POOL_GROUPS
DIFF_HEAD_DIM = 64
N_DIFF_HEADS = D_ATTN // (2 * DIFF_HEAD_DIM)
V_HEAD_DIM = 2 * DIFF_HEAD_DIM
ROPE_PAIRS = DIFF_HEAD_DIM // 4
ROPE_BASE = 10000.0
Q_BLOCK = 128
EPS = 1e-6
O_POOL_V = 0
O_POOL_G = O_POOL_V + D_POOL
O_Q = O_POOL_G + D_POOL
O_K = O_Q + D_ATTN
O_V = O_K + D_ATTN
O_ATTN_G = O_V + D_ATTN
D_IN = O_ATTN_G + D_ATTN

kernel_name = 'hybrid_pool_diffattn_prefix_block'


def rms_norm(x, w):
    xf = x.astype(jnp.float32)
    y = xf * lax.rsqrt(jnp.mean(xf * xf, axis=-1, keepdims=True) + EPS)
    return (y * w.astype(jnp.float32)).astype(x.dtype)


def adaln_params(cond, w_ada, b_ada):
    mod = jax.nn.silu(cond) @ w_ada + b_ada
    shift, scale, gate = jnp.split(mod, 3, axis=-1)
    return shift[:, None, :], scale[:, None, :], gate[:, None, :]


def multiscale_pool(v, pool_w, pool_scale):
    B, L, _ = v.shape
    vf = v.astype(jnp.float32)
    csum = jnp.concatenate([jnp.zeros((B, 1, D_POOL), jnp.float32), jnp.cumsum(vf, axis=1)], axis=1)
    t = jnp.arange(L)
    outs = []
    for g, w in enumerate(POOL_WINDOWS):
        sl = slice(g * POOL_GROUP, (g + 1) * POOL_GROUP)
        lo = jnp.clip(t - w // 2, 0, L)
        hi = jnp.clip(t + w // 2, 0, L)
        cs = csum[:, :, sl]
        cnt = (hi - lo).astype(jnp.float32)[None, :, None]
        mean = (jnp.take(cs, hi, axis=1) - jnp.take(cs, lo, axis=1)) / cnt
        outs.append(mean - vf[:, :, sl])
    pooled = jnp.stack(outs, axis=2)
    mixed = jnp.einsum('blgc,gcd->blgd', pooled, pool_w.astype(jnp.float32))
    return (mixed.reshape(B, L, D_POOL) * pool_scale.astype(jnp.float32)).astype(v.dtype)


def axial_rope_tables(L, dtype):
    rows = L // GRID_W
    row = jnp.broadcast_to(jnp.arange(rows)[:, None], (rows, GRID_W)).reshape(-1).astype(jnp.float32)
    col = jnp.broadcast_to(jnp.arange(GRID_W)[None, :], (rows, GRID_W)).reshape(-1).astype(jnp.float32)
    inv_freq = ROPE_BASE ** (-jnp.arange(ROPE_PAIRS, dtype=jnp.float32) / ROPE_PAIRS)
    ang_r = row[:, None] * inv_freq
    ang_c = col[:, None] * inv_freq
    shape = (1, L, 1, 1, ROPE_PAIRS)
    cos_r = jnp.cos(ang_r).reshape(shape).astype(dtype)
    sin_r = jnp.sin(ang_r).reshape(shape).astype(dtype)
    cos_c = jnp.cos(ang_c).reshape(shape).astype(dtype)
    sin_c = jnp.sin(ang_c).reshape(shape).astype(dtype)
    return (cos_r, sin_r, cos_c, sin_c)


def rotate_half(x, cos, sin):
    x1, x2 = jnp.split(x, 2, axis=-1)
    return jnp.concatenate([x1 * cos - x2 * sin, x2 * cos + x1 * sin], axis=-1)


def apply_axial_rope(x, tables):
    cos_r, sin_r, cos_c, sin_c = tables
    half = DIFF_HEAD_DIM // 2
    return jnp.concatenate([rotate_half(x[..., :half], cos_r, sin_r),
                            rotate_half(x[..., half:], cos_c, sin_c)], axis=-1)


def diff_attend(q, k, v, lam):
    s = jnp.einsum('bqhnd,bkhnd->bhnqk', q, k).astype(jnp.float32) * (DIFF_HEAD_DIM ** -0.5)
    p = jax.nn.softmax(s, axis=-1)
    a = p[:, :, 0] - lam * p[:, :, 1]
    return jnp.einsum('bhqk,bkhe->bqhe', a.astype(v.dtype), v)


def merge_branches(p, o_attn, lam_init, pool_w, pool_scale, subln_w, w_out):
    B, L = p.shape[:2]
    pool_o = multiscale_pool(p[..., O_POOL_V:O_POOL_G], pool_w, pool_scale) * jax.nn.silu(p[..., O_POOL_G:O_Q])
    attn_o = (rms_norm(o_attn, subln_w).reshape(B, L, D_ATTN) * (1.0 - lam_init)
              * jax.nn.silu(p[..., O_ATTN_G:D_IN]))
    return jnp.concatenate([pool_o, attn_o], axis=-1) @ w_out


def hybrid_layer(x, ctx, c, c_ctx, rope_tables, layer_idx, update_ctx,
                 norm_w, w_ada, b_ada, w_in, pool_w, pool_scale, q_norm_w, k_norm_w,
                 lambda_q1, lambda_k1, lambda_q2, lambda_k2, subln_w, w_out):
    B, L, _ = x.shape
    C = ctx.shape[1]
    f32 = jnp.float32
    lam_init = 0.8 - 0.6 * math.exp(-0.3 * layer_idx)
    lam = (jnp.exp(jnp.sum(lambda_q1.astype(f32) * lambda_k1.astype(f32)))
           - jnp.exp(jnp.sum(lambda_q2.astype(f32) * lambda_k2.astype(f32))) + lam_init)

    shift, scale, gate = adaln_params(c, w_ada, b_ada)
    shift_c, scale_c, gate_c = adaln_params(c_ctx[None, :], w_ada, b_ada)
    h = rms_norm(x, norm_w) * (1 + scale) + shift
    hc = rms_norm(ctx, norm_w) * (1 + scale_c) + shift_c

    ctx_cols = slice(0, D_IN) if update_ctx else slice(O_K, O_ATTN_G)
    base = ctx_cols.start
    pc = hc @ w_in[:, ctx_cols]
    kc = rms_norm(pc[..., O_K - base:O_V - base].reshape(B, C, N_DIFF_HEADS, 2, DIFF_HEAD_DIM), k_norm_w)
    vc = pc[..., O_V - base:O_ATTN_G - base].reshape(B, C, N_DIFF_HEADS, V_HEAD_DIM)

    p = h @ w_in
    q = apply_axial_rope(rms_norm(p[..., O_Q:O_K].reshape(B, L, N_DIFF_HEADS, 2, DIFF_HEAD_DIM), q_norm_w), rope_tables)
    k = apply_axial_rope(rms_norm(p[..., O_K:O_V].reshape(B, L, N_DIFF_HEADS, 2, DIFF_HEAD_DIM), k_norm_w), rope_tables)
    v = p[..., O_V:O_ATTN_G].reshape(B, L, N_DIFF_HEADS, V_HEAD_DIM)
    k_all = jnp.concatenate([kc, k], axis=1)
    v_all = jnp.concatenate([vc, v], axis=1)
    nb = L // Q_BLOCK
    qb = q.reshape(B, nb, Q_BLOCK, N_DIFF_HEADS, 2, DIFF_HEAD_DIM).swapaxes(0, 1)
    o = lax.map(lambda qi: diff_attend(qi, k_all, v_all, lam), qb)
    o = o.swapaxes(0, 1).reshape(B, L, N_DIFF_HEADS, V_HEAD_DIM)
    x_new = x + gate * merge_branches(p, o, lam_init, pool_w, pool_scale, subln_w, w_out)

    if update_ctx:
        qc = rms_norm(pc[..., O_Q:O_K].reshape(B, C, N_DIFF_HEADS, 2, DIFF_HEAD_DIM), q_norm_w)
        oc = diff_attend(qc, kc, vc, lam)
        ctx_new = ctx + gate_c * merge_branches(pc, oc, lam_init, pool_w, pool_scale, subln_w, w_out)
    else:
        ctx_new = ctx
    return x_new, ctx_new


def setup_inputs(seed: int = 0) -> dict:
    key = jax.random.key(seed)
    ks = jax.random.split(key, 18)
    f32 = jnp.float32

    def nrm(k, shape, s):
        return jax.random.normal(k, shape, f32) * s

    def gain(k, shape):
        return 1.0 + 0.02 * jax.random.normal(k, shape, f32)

    return {
        'x': nrm(ks[0], (BATCH, SEQ, D_MODEL), 1.0),
        'c': nrm(ks[1], (BATCH, D_MODEL), 1.0),
        'ctx': nrm(ks[2], (BATCH, CTX_LEN, D_MODEL), 1.0),
        'c_ctx': nrm(ks[3], (D_MODEL,), 1.0),
        'norm_w': gain(ks[4], (DEPTH, D_MODEL)),
        'w_ada': nrm(ks[5], (DEPTH, D_MODEL, 3 * D_MODEL), D_MODEL ** -0.5),
        'b_ada': nrm(ks[6], (DEPTH, 3 * D_MODEL), 0.02),
        'w_in': nrm(ks[7], (DEPTH, D_MODEL, D_IN), D_MODEL ** -0.5),
        'pool_w': nrm(ks[8], (DEPTH, N_POOL_GROUPS, POOL_GROUP, POOL_GROUP), POOL_GROUP ** -0.5),
        'pool_scale': gain(ks[9], (DEPTH, D_POOL)),
        'q_norm_w': gain(ks[10], (DEPTH, DIFF_HEAD_DIM)),
        'k_norm_w': gain(ks[11], (DEPTH, DIFF_HEAD_DIM)),
        'lambda_q1': nrm(ks[12], (DEPTH, DIFF_HEAD_DIM), 0.1),
        'lambda_k1': nrm(ks[13], (DEPTH, DIFF_HEAD_DIM), 0.1),
        'lambda_q2': nrm(ks[14], (DEPTH, DIFF_HEAD_DIM), 0.1),
        'lambda_k2': nrm(ks[15], (DEPTH, DIFF_HEAD_DIM), 0.1),
        'subln_w': gain(ks[16], (DEPTH, V_HEAD_DIM)),
        'w_out': nrm(ks[17], (DEPTH, D_MIX, D_MODEL), D_MIX ** -0.5),
    }


def reference(x, c, ctx, c_ctx, norm_w, w_ada, b_ada, w_in, pool_w, pool_scale,
              q_norm_w, k_norm_w, lambda_q1, lambda_k1, lambda_q2, lambda_k2, subln_w, w_out):
    rope_tables = axial_rope_tables(x.shape[1], x.dtype)
    for l in range(DEPTH):
        x, ctx = hybrid_layer(x, ctx, c, c_ctx, rope_tables, l, l < DEPTH - 1,
                              norm_w[l], w_ada[l], b_ada[l], w_in[l], pool_w[l], pool_scale[l],
                              q_norm_w[l], k_norm_w[l], lambda_q1[l], lambda_k1[l],
                              lambda_q2[l], lambda_k2[l], subln_w[l], w_out[l])
    return x
```

```python
import functools
import math

import numpy as np
import jax
import jax.numpy as jnp
from jax import lax
from jax.experimental import pallas as pl
from jax.experimental.pallas import tpu as pltpu

F32 = jnp.float32
BF16 = jnp.bfloat16

GRID_W = 64
POOL_WINDOWS = (2, 4, 8, 16)
N_POOL_GROUPS = len(POOL_WINDOWS)
DIFF_HEAD_DIM = 64
V_HEAD_DIM = 2 * DIFF_HEAD_DIM
ROPE_PAIRS = DIFF_HEAD_DIM // 4
ROPE_BASE = 10000.0
EPS = 1e-6

LANES = 128
BF16_SUBLANES = 16
MXU_DIM = 256
VMEM_LIMIT_BYTES = 56 * 1024 * 1024
COND_ROWS = 16
POOL_HALO = BF16_SUBLANES


def _params(n_axes):
    return pltpu.CompilerParams(
        dimension_semantics=("arbitrary",) * n_axes,
        vmem_limit_bytes=VMEM_LIMIT_BYTES)


def _silu(a):
    return a * jax.nn.sigmoid(a)


def _adaln_body(cond_ref, w_ref, b_ref, o_ref):
    s = _silu(cond_ref[...]).astype(BF16)
    o_ref[...] = jnp.dot(s, w_ref[...].astype(BF16), preferred_element_type=F32) + b_ref[...]


def _adaln(cond, w_ada, b_ada):
    depth, d, n = w_ada.shape
    tn = 512
    return pl.pallas_call(
        _adaln_body,
        grid=(depth, n // tn),
        in_specs=[
            pl.BlockSpec((COND_ROWS, d), lambda l, j: (0, 0)),
            pl.BlockSpec((None, d, tn), lambda l, j: (l, 0, j)),
            pl.BlockSpec((None, 1, tn), lambda l, j: (l, 0, j)),
        ],
        out_specs=pl.BlockSpec((None, COND_ROWS, tn), lambda l, j: (l, 0, j)),
        out_shape=jax.ShapeDtypeStruct((depth, COND_ROWS, n), F32),
        compiler_params=_params(2),
        name="adaln",
    )(cond, w_ada, b_ada.reshape(depth, 1, n))


def _norm_body(x_ref, nw_ref, shift_ref, scale_ref, o_ref):
    x = x_ref[...]
    y = x * lax.rsqrt(jnp.mean(x * x, axis=-1, keepdims=True) + EPS) * nw_ref[...]
    o_ref[...] = (y * (1.0 + scale_ref[...]) + shift_ref[...]).astype(BF16)


def _norm_modulate(x, norm_w, mod4, layer, cond_row, tm):
    nb, s, d = x.shape
    nt = s // tm
    return pl.pallas_call(
        _norm_body,
        grid=(nb, nt),
        in_specs=[
            pl.BlockSpec((None, tm, d), lambda b, i: (b, i, 0)),
            pl.BlockSpec((None, 1, d), lambda b, i: (layer, 0, 0)),
            pl.BlockSpec((None, None, 1, d), lambda b, i: (layer, cond_row(b), 0, 0)),
            pl.BlockSpec((None, None, 1, d), lambda b, i: (layer, cond_row(b), 0, 1)),
        ],
        out_specs=pl.BlockSpec((tm, d), lambda b, i: (b * nt + i, 0)),
        out_shape=jax.ShapeDtypeStruct((nb * s, d), BF16),
        compiler_params=_params(2),
        name="norm_modulate",
    )(x, norm_w.reshape(norm_w.shape[0], 1, d), mod4, mod4)


REGION = 1024
PROJ_CHUNK = 512


def _group_mean_matrix():
    g = np.arange(MXU_DIM) // DIFF_HEAD_DIM
    return jnp.asarray((g[:, None] == g[None, :]).astype(np.float32) / DIFF_HEAD_DIM, dtype=BF16)


def _qk_epilogue(acc, gm, nw, cos, sin, out_scale):
    outs = []
    for s0 in range(0, PROJ_CHUNK, MXU_DIM):
        a = acc[:, s0:s0 + MXU_DIM]
        sq = a * a
        hi = sq.astype(BF16)
        lo = (sq - hi.astype(F32)).astype(BF16)
        ms = (jnp.dot(hi, gm, preferred_element_type=F32)
              + jnp.dot(lo, gm, preferred_element_type=F32))
        y = a * lax.rsqrt(ms + EPS)
        for h0 in range(0, MXU_DIM, LANES):
            yh = y[:, h0:h0 + LANES] * nw
            if cos is not None:
                lane = lax.broadcasted_iota(jnp.int32, yh.shape, 1)
                first = (lane & (2 * ROPE_PAIRS - 1)) < ROPE_PAIRS
                partner = jnp.where(first,
                                    pltpu.roll(yh, LANES - ROPE_PAIRS, axis=1),
                                    pltpu.roll(yh, ROPE_PAIRS, axis=1))
                yh = yh * cos + partner * sin
            if out_scale != 1.0:
                yh = yh * out_scale
            outs.append(yh.astype(BF16))
    return outs


def _inproj_body(*refs, kinds, has_qk, has_rope):
    h_ref, w_ref = refs[0], refs[1]
    pos = 2
    gm = nws = cos = sin = None
    if has_qk:
        gm = refs[pos][...]
        pos += 1
        nws = []
        for kind in kinds:
            if kind[0] == "qk":
                nws.append(refs[pos][...])
                pos += 1
            else:
                nws.append(None)
    if has_rope:
        cos = refs[pos][...]
        sin = refs[pos + 1][...]
        pos += 2
    out_refs = refs[pos:]
    h = h_ref[...]
    for r, kind in enumerate(kinds):
        for c0 in range(0, REGION, PROJ_CHUNK):
            col = r * REGION + c0
            acc = jnp.dot(h, w_ref[:, col:col + PROJ_CHUNK], preferred_element_type=F32)
            if kind[0] == "plain":
                out_refs[r][:, c0:c0 + PROJ_CHUNK] = acc.astype(BF16)
            elif kind[0] == "silu":
                out_refs[r][:, c0:c0 + PROJ_CHUNK] = _silu(acc).astype(BF16)
            else:
                _, rope, out_scale = kind
                pieces = _qk_epilogue(acc, gm, nws[r],
                                      cos if rope else None, sin if rope else None, out_scale)
                for i, piece in enumerate(pieces):
                    out_refs[r][:, c0 + i * LANES:c0 + (i + 1) * LANES] = piece


def _inproj(h, w_in_bf, layer, col_start, kinds, norm_ws, rope_tabs, tm, tiles_per_seq):
    m, d = h.shape
    ncols = REGION * len(kinds)
    has_qk = any(k[0] == "qk" for k in kinds)
    has_rope = any(k[0] == "qk" and k[1] for k in kinds)
    in_specs = [
        pl.BlockSpec((tm, d), lambda i: (i, 0)),
        pl.BlockSpec((None, d, ncols), lambda i: (layer, 0, col_start // ncols)),
    ]
    args = [h, w_in_bf]
    if has_qk:
        in_specs.append(pl.BlockSpec((MXU_DIM, MXU_DIM), lambda i: (0, 0)))
        args.append(_group_mean_matrix())
        for nw in norm_ws:
            if nw is not None:
                in_specs.append(pl.BlockSpec((1, LANES), lambda i: (0, 0)))
                args.append(jnp.tile(nw.astype(F32), LANES // DIFF_HEAD_DIM).reshape(1, LANES))
    if has_rope:
        for tab in rope_tabs:
            in_specs.append(pl.BlockSpec((tm, LANES), lambda i: (i % tiles_per_seq, 0)))
            args.append(tab)
    outs = pl.pallas_call(
        functools.partial(_inproj_body, kinds=kinds, has_qk=has_qk, has_rope=has_rope),
        grid=(m // tm,),
        in_specs=in_specs,
        out_specs=[pl.BlockSpec((tm, REGION), lambda i: (i, 0)) for _ in kinds],
        out_shape=[jax.ShapeDtypeStruct((m, REGION), BF16) for _ in kinds],
        compiler_params=_params(1),
        name="inproj",
    )(*args)
    return outs


def _attn_body(*refs, lam_init, n_lat_chunks, tk, has_lat):
    lq1, lk1, lq2, lk2, sw_ref, q_ref = refs[:6]
    if has_lat:
        k_ref, v_ref, kc_ref, vc_ref, o_ref = refs[6:]
    else:
        kc_ref, vc_ref, o_ref = refs[6:]
    q = q_ref[...]
    tq = q.shape[0]
    lane = lax.broadcasted_iota(jnp.int32, q.shape, 1)
    zero = jnp.zeros_like(q)
    qs = jnp.concatenate([jnp.where(lane < DIFF_HEAD_DIM, q, zero),
                          jnp.where(lane >= DIFF_HEAD_DIM, q, zero)], axis=0)

    def step(k, v, carry):
        m, l, acc = carry
        s = lax.dot_general(qs, k, (((1,), (1,)), ((), ())), preferred_element_type=F32)
        m_new = jnp.maximum(m, jnp.max(s, axis=-1, keepdims=True))
        alpha = jnp.exp(m - m_new)
        p = jnp.exp(s - m_new)
        l = alpha * l + jnp.sum(p, axis=-1, keepdims=True)
        acc = alpha * acc + jnp.dot(p.astype(BF16), v, preferred_element_type=F32)
        return m_new, l, acc

    carry = (jnp.full((2 * tq, 1), -jnp.inf, F32),
             jnp.zeros((2 * tq, 1), F32),
             jnp.zeros((2 * tq, V_HEAD_DIM), F32))
    if has_lat:
        for c in range(n_lat_chunks):
            carry = step(k_ref[c * tk:(c + 1) * tk, :], v_ref[c * tk:(c + 1) * tk, :], carry)
    carry = step(kc_ref[...], vc_ref[...], carry)
    _, l, acc = carry
    o = acc / l
    lam = (jnp.exp(jnp.sum(lq1[...] * lk1[...], axis=-1, keepdims=True))
           - jnp.exp(jnp.sum(lq2[...] * lk2[...], axis=-1, keepdims=True)) + lam_init)
    od = o[:tq] - lam * o[tq:]
    y = od * lax.rsqrt(jnp.mean(od * od, axis=-1, keepdims=True) + EPS) * sw_ref[...]
    o_ref[...] = (y * (1.0 - lam_init)).astype(BF16)


def _attention(q, k, v, kc, vc, lams, subln_w, lam_init, nb, sq, sk, sc, tq, tk):
    n_heads = q.shape[1] // V_HEAD_DIM
    nq = sq // tq
    has_lat = k is not None
    small = pl.BlockSpec((1, DIFF_HEAD_DIM), lambda b, h, i: (0, 0))
    in_specs = [small, small, small, small,
                pl.BlockSpec((1, V_HEAD_DIM), lambda b, h, i: (0, 0)),
                pl.BlockSpec((tq, V_HEAD_DIM), lambda b, h, i: (b * nq + i, h))]
    args = [a.astype(F32).reshape(1, DIFF_HEAD_DIM) for a in lams]
    args += [subln_w.astype(F32).reshape(1, V_HEAD_DIM), q]
    if has_lat:
        in_specs += [pl.BlockSpec((sk, V_HEAD_DIM), lambda b, h, i: (b, h))] * 2
        args += [k, v]
    in_specs += [pl.BlockSpec((sc, V_HEAD_DIM), lambda b, h, i: (b, h))] * 2
    args += [kc, vc]
    return pl.pallas_call(
        functools.partial(_attn_body, lam_init=lam_init,
                          n_lat_chunks=(sk // tk if has_lat else 0), tk=tk, has_lat=has_lat),
        grid=(nb, n_heads, nq),
        in_specs=in_specs,
        out_specs=pl.BlockSpec((tq, V_HEAD_DIM), lambda b, h, i: (b * nq + i, h)),
        out_shape=jax.ShapeDtypeStruct(q.shape, BF16),
        compiler_params=_params(3),
        name="diff_attention",
    )(*args)


def _band_matrices(tm):
    t = np.arange(tm)[:, None]
    j = np.arange(tm + 2 * POOL_HALO)[None, :] - POOL_HALO
    mats = [((j >= t - w // 2) & (j < t + w // 2)).astype(np.float32) for w in POOL_WINDOWS]
    return jnp.asarray(np.stack(mats), dtype=BF16)


def _merge_body(pv_ref, prev_ref, next_ref, pg_ref, o_ref, ag_ref, x_ref, gate_ref,
                band_ref, pw_ref, ps_ref, wo_ref, out_ref, *, tiles_per_seq, seq_len):
    i = pl.program_id(0)
    tm = pv_ref.shape[0]
    ti = i % tiles_per_seq
    main = pv_ref[...]
    zero_halo = jnp.zeros_like(prev_ref[...])
    prev = jnp.where(ti == 0, zero_halo, prev_ref[...])
    nxt = jnp.where(ti == tiles_per_seq - 1, zero_halo, next_ref[...])
    ext = jnp.concatenate([prev, main, nxt], axis=0)
    tpos = ti * tm + lax.broadcasted_iota(jnp.int32, (tm, 1), 0)
    gw = main.shape[1] // N_POOL_GROUPS
    d_pool = main.shape[1]
    y = None
    for g, w in enumerate(POOL_WINDOWS):
        sl = slice(g * gw, (g + 1) * gw)
        wsum = jnp.dot(band_ref[g], ext[:, sl], preferred_element_type=F32)
        cnt = (jnp.minimum(tpos + w // 2, seq_len) - jnp.maximum(tpos - w // 2, 0)).astype(F32)
        pooled = wsum / cnt - main[:, sl].astype(F32)
        mixed = jnp.dot(pooled.astype(BF16), pw_ref[g], preferred_element_type=F32)
        pool_o = (mixed * ps_ref[:, sl] * pg_ref[:, sl].astype(F32)).astype(BF16)
        part = jnp.dot(pool_o, wo_ref[g * gw:(g + 1) * gw, :], preferred_element_type=F32)
        y = part if y is None else y + part
    attn_o = (o_ref[...].astype(F32) * ag_ref[...].astype(F32)).astype(BF16)
    y = y + jnp.dot(attn_o, wo_ref[d_pool:, :], preferred_element_type=F32)
    out_ref[...] = x_ref[...] + gate_ref[...] * y


def _merge(pool_v, pool_g, o, attn_g, x2, mod4, layer, cond_row, pool_w_bf, pool_scale, w_out_bf,
           seq_len, tm):
    m, d_pool = pool_v.shape
    d = x2.shape[1]
    tiles_per_seq = seq_len // tm
    hb = tm // POOL_HALO
    n_halo_blocks = m // POOL_HALO
    row = lambda i: (i, 0)
    in_specs = [
        pl.BlockSpec((tm, d_pool), row),
        pl.BlockSpec((POOL_HALO, d_pool), lambda i: (jnp.maximum(i * hb - 1, 0), 0)),
        pl.BlockSpec((POOL_HALO, d_pool), lambda i: (jnp.minimum((i + 1) * hb, n_halo_blocks - 1), 0)),
        pl.BlockSpec((tm, d_pool), row),
        pl.BlockSpec((tm, d_pool), row),
        pl.BlockSpec((tm, d_pool), row),
        pl.BlockSpec((tm, d), row),
        pl.BlockSpec((None, None, 1, d), lambda i: (layer, cond_row(i // tiles_per_seq), 0, 2)),
        pl.BlockSpec((N_POOL_GROUPS, tm, tm + 2 * POOL_HALO), lambda i: (0, 0, 0)),
        pl.BlockSpec((None, N_POOL_GROUPS, d_pool // N_POOL_GROUPS, d_pool // N_POOL_GROUPS),
                     lambda i: (layer, 0, 0, 0)),
        pl.BlockSpec((None, 1, d_pool), lambda i: (layer, 0, 0)),
        pl.BlockSpec((None, 2 * d_pool, d), lambda i: (layer, 0, 0)),
    ]
    return pl.pallas_call(
        functools.partial(_merge_body, tiles_per_seq=tiles_per_seq, seq_len=seq_len),
        grid=(m // tm,),
        in_specs=in_specs,
        out_specs=pl.BlockSpec((tm, d), row),
        out_shape=jax.ShapeDtypeStruct((m, d), F32),
        compiler_params=_params(1),
        name="merge",
    )(pool_v, pool_v, pool_v, pool_g, o, attn_g, x2, mod4, _band_matrices(tm), pool_w_bf,
      pool_scale.reshape(pool_scale.shape[0], 1, d_pool), w_out_bf)


def _rope_tables(seq_len):
    rows = seq_len // GRID_W
    row = jnp.broadcast_to(jnp.arange(rows)[:, None], (rows, GRID_W)).reshape(-1).astype(F32)
    col = jnp.broadcast_to(jnp.arange(GRID_W)[None, :], (rows, GRID_W)).reshape(-1).astype(F32)
    inv_freq = ROPE_BASE ** (-jnp.arange(ROPE_PAIRS, dtype=F32) / ROPE_PAIRS)
    ang_r = row[:, None] * inv_freq
    ang_c = col[:, None] * inv_freq
    cos64 = jnp.concatenate([jnp.cos(ang_r), jnp.cos(ang_r), jnp.cos(ang_c), jnp.cos(ang_c)], axis=-1)
    sin64 = jnp.concatenate([-jnp.sin(ang_r), jnp.sin(ang_r), -jnp.sin(ang_c), jnp.sin(ang_c)], axis=-1)
    return jnp.tile(cos64, (1, 2)), jnp.tile(sin64, (1, 2))


def kernel(x, c, ctx, c_ctx, norm_w, w_ada, b_ada, w_in, pool_w, pool_scale, q_norm_w, k_norm_w,
           lambda_q1, lambda_k1, lambda_q2, lambda_k2, subln_w, w_out):
    nb, seq, d = x.shape
    n_ctx = ctx.shape[1]
    depth = w_in.shape[0]
    d_pool = pool_scale.shape[1]
    o_pool_v, o_q, o_v = 0, 2 * d_pool, 2 * d_pool + 2 * REGION
    o_k = o_q + REGION

    cond = jnp.concatenate([c, c_ctx[None, :]], axis=0)
    cond = jnp.pad(cond, ((0, COND_ROWS - cond.shape[0]), (0, 0)))
    mod = _adaln(cond, w_ada, b_ada)
    mod4 = mod.reshape(depth, COND_ROWS, 1, 3 * d)
    w_in_bf = w_in.astype(BF16)
    w_out_bf = w_out.astype(BF16)
    pool_w_bf = pool_w.astype(BF16)
    rope_tabs = _rope_tables(seq)
    q_scale = DIFF_HEAD_DIM ** -0.5

    lat_row = lambda b: b
    ctx_row = lambda b: nb
    tm_lat, tm_ctx = 1024, n_ctx
    tq, tk = 512, 512

    x2 = x.reshape(nb * seq, d)
    ctx2 = ctx.reshape(nb * n_ctx, d)
    for l in range(depth):
        update_ctx = l < depth - 1
        lam_init = 0.8 - 0.6 * math.exp(-0.3 * l)
        lams = (lambda_q1[l], lambda_k1[l], lambda_q2[l], lambda_k2[l])
        qn, kn = q_norm_w[l], k_norm_w[l]

        h = _norm_modulate(x2.reshape(nb, seq, d), norm_w, mod4, l, lat_row, 512)
        hc = _norm_modulate(ctx2.reshape(nb, n_ctx, d), norm_w, mod4, l, ctx_row, n_ctx)

        lat = functools.partial(_inproj, h, w_in_bf, l, tm=tm_lat, tiles_per_seq=seq // tm_lat)
        pool_v, pool_g = lat(o_pool_v, (("plain",), ("silu",)), None, None)
        q, k = lat(o_q, (("qk", True, q_scale), ("qk", True, 1.0)), (qn, kn), rope_tabs)
        v, attn_g = lat(o_v, (("plain",), ("silu",)), None, None)

        cx = functools.partial(_inproj, hc, w_in_bf, l, tm=tm_ctx, tiles_per_seq=1)
        if update_ctx:
            pool_vc, pool_gc = cx(o_pool_v, (("plain",), ("silu",)), None, None)
            qc, kc = cx(o_q, (("qk", False, q_scale), ("qk", False, 1.0)), (qn, kn), None)
            vc, attn_gc = cx(o_v, (("plain",), ("silu",)), None, None)
        else:
            (kc,) = cx(o_k, (("qk", False, 1.0),), (kn,), None)
            (vc,) = cx(o_v, (("plain",),), None, None)

        o = _attention(q, k, v, kc, vc, lams, subln_w[l], lam_init, nb, seq, seq, n_ctx, tq, tk)
        x2 = _merge(pool_v, pool_g, o, attn_g, x2, mod4, l, lat_row, pool_w_bf, pool_scale,
                    w_out_bf, seq, 512)
        if update_ctx:
            oc = _attention(qc, None, None, kc, vc, lams, subln_w[l], lam_init,
                            nb, n_ctx, 0, n_ctx, n_ctx, tk)
            ctx2 = _merge(pool_vc, pool_gc, oc, attn_gc, ctx2, mod4, l, ctx_row, pool_w_bf,
                          pool_scale, w_out_bf, n_ctx, n_ctx)
    return x2.reshape(nb, seq, d)
```

```python
import functools
import math

import numpy as np
import jax
import jax.numpy as jnp
from jax import lax
from jax.experimental import pallas as pl
from jax.experimental.pallas import tpu as pltpu

F32 = jnp.float32
BF16 = jnp.bfloat16

GRID_W = 64
POOL_WINDOWS = (2, 4, 8, 16)
N_POOL_GROUPS = len(POOL_WINDOWS)
DIFF_HEAD_DIM = 64
V_HEAD_DIM = 2 * DIFF_HEAD_DIM
ROPE_PAIRS = DIFF_HEAD_DIM // 4
ROPE_BASE = 10000.0
EPS = 1e-6

LANES = 128
BF16_SUBLANES = 16
MXU_DIM = 256
VMEM_LIMIT_BYTES = 56 * 1024 * 1024
COND_ROWS = 16
POOL_HALO = BF16_SUBLANES


def _params(n_axes):
    return pltpu.CompilerParams(
        dimension_semantics=("arbitrary",) * n_axes,
        vmem_limit_bytes=VMEM_LIMIT_BYTES)


def _silu(a):
    return a * jax.nn.sigmoid(a)


def _adaln_body(cond_ref, w_ref, b_ref, o_ref):
    s = _silu(cond_ref[...]).astype(BF16)
    o_ref[...] = jnp.dot(s, w_ref[...].astype(BF16), preferred_element_type=F32) + b_ref[...]


def _adaln(cond, w_ada, b_ada):
    depth, d, n = w_ada.shape
    tn = 512
    return pl.pallas_call(
        _adaln_body,
        grid=(depth, n // tn),
        in_specs=[
            pl.BlockSpec((COND_ROWS, d), lambda l, j: (0, 0)),
            pl.BlockSpec((None, d, tn), lambda l, j: (l, 0, j)),
            pl.BlockSpec((None, 1, tn), lambda l, j: (l, 0, j)),
        ],
        out_specs=pl.BlockSpec((None, COND_ROWS, tn), lambda l, j: (l, 0, j)),
        out_shape=jax.ShapeDtypeStruct((depth, COND_ROWS, n), F32),
        compiler_params=_params(2),
        name="adaln",
    )(cond, w_ada, b_ada.reshape(depth, 1, n))


def _norm_body(x_ref, nw_ref, shift_ref, scale_ref, o_ref):
    x = x_ref[...]
    y = x * lax.rsqrt(jnp.mean(x * x, axis=-1, keepdims=True) + EPS) * nw_ref[...]
    o_ref[...] = (y * (1.0 + scale_ref[...]) + shift_ref[...]).astype(BF16)


def _norm_modulate(x, norm_w, mod4, layer, cond_row, tm):
    nb, s, d = x.shape
    nt = s // tm
    return pl.pallas_call(
        _norm_body,
        grid=(nb, nt),
        in_specs=[
            pl.BlockSpec((None, tm, d), lambda b, i: (b, i, 0)),
            pl.BlockSpec((None, 1, d), lambda b, i: (layer, 0, 0)),
            pl.BlockSpec((None, None, 1, d), lambda b, i: (layer, cond_row(b), 0, 0)),
            pl.BlockSpec((None, None, 1, d), lambda b, i: (layer, cond_row(b), 0, 1)),
        ],
        out_specs=pl.BlockSpec((tm, d), lambda b, i: (b * nt + i, 0)),
        out_shape=jax.ShapeDtypeStruct((nb * s, d), BF16),
        compiler_params=_params(2),
        name="norm_modulate",
    )(x, norm_w.reshape(norm_w.shape[0], 1, d), mod4, mod4)


REGION = 1024
PROJ_CHUNK = 512


def _group_mean_matrix():
    g = np.arange(MXU_DIM) // DIFF_HEAD_DIM
    return jnp.asarray((g[:, None] == g[None, :]).astype(np.float32) / DIFF_HEAD_DIM, dtype=BF16)


def _qk_epilogue(acc, gm, nw, cos, sin, out_scale):
    outs = []
    for s0 in range(0, PROJ_CHUNK, MXU_DIM):
        a = acc[:, s0:s0 + MXU_DIM]
        sq = a * a
        hi = sq.astype(BF16)
        lo = (sq - hi.astype(F32)).astype(BF16)
        ms = (jnp.dot(hi, gm, preferred_element_type=F32)
              + jnp.dot(lo, gm, preferred_element_type=F32))
        y = a * lax.rsqrt(ms + EPS)
        for h0 in range(0, MXU_DIM, LANES):
            yh = y[:, h0:h0 + LANES] * nw
            if cos is not None:
                lane = lax.broadcasted_iota(jnp.int32, yh.shape, 1)
                first = (lane & (2 * ROPE_PAIRS - 1)) < ROPE_PAIRS
                partner = jnp.where(first,
                                    pltpu.roll(yh, LANES - ROPE_PAIRS, axis=1),
                                    pltpu.roll(yh, ROPE_PAIRS, axis=1))
                yh = yh * cos + partner * sin
            if out_scale != 1.0:
                yh = yh * out_scale
            outs.append(yh.astype(BF16))
    return outs


def _inproj_body(*refs, kinds, has_qk, has_rope):
    h_ref, w_ref = refs[0], refs[1]
    pos = 2
    gm = nws = cos = sin = None
    if has_qk:
        gm = refs[pos][...]
        pos += 1
        nws = []
        for kind in kinds:
            if kind[0] == "qk":
                nws.append(refs[pos][...])
                pos += 1
            else:
                nws.append(None)
    if has_rope:
        cos = refs[pos][...]
        sin = refs[pos + 1][...]
        pos += 2
    out_refs = refs[pos:]
    h = h_ref[...]
    for r, kind in enumerate(kinds):
        for c0 in range(0, REGION, PROJ_CHUNK):
            col = r * REGION + c0
            acc = jnp.dot(h, w_ref[:, col:col + PROJ_CHUNK], preferred_element_type=F32)
            if kind[0] == "plain":
                out_refs[r][:, c0:c0 + PROJ_CHUNK] = acc.astype(BF16)
            elif kind[0] == "silu":
                out_refs[r][:, c0:c0 + PROJ_CHUNK] = _silu(acc).astype(BF16)
            else:
                _, rope, out_scale = kind
                pieces = _qk_epilogue(acc, gm, nws[r],
                                      cos if rope else None, sin if rope else None, out_scale)
                for i, piece in enumerate(pieces):
                    out_refs[r][:, c0 + i * LANES:c0 + (i + 1) * LANES] = piece


def _inproj(h, w_in_bf, layer, col_start, kinds, norm_ws, rope_tabs, tm, tiles_per_seq):
    m, d = h.shape
    ncols = REGION * len(kinds)
    has_qk = any(k[0] == "qk" for k in kinds)
    has_rope = any(k[0] == "qk" and k[1] for k in kinds)
    in_specs = [
        pl.BlockSpec((tm, d), lambda i: (i, 0)),
        pl.BlockSpec((None, d, ncols), lambda i: (layer, 0, col_start // ncols)),
    ]
    args = [h, w_in_bf]
    if has_qk:
        in_specs.append(pl.BlockSpec((MXU_DIM, MXU_DIM), lambda i: (0, 0)))
        args.append(_group_mean_matrix())
        for nw in norm_ws:
            if nw is not None:
                in_specs.append(pl.BlockSpec((1, LANES), lambda i: (0, 0)))
                args.append(jnp.tile(nw.astype(F32), LANES // DIFF_HEAD_DIM).reshape(1, LANES))
    if has_rope:
        for tab in rope_tabs:
            in_specs.append(pl.BlockSpec((tm, LANES), lambda i: (i % tiles_per_seq, 0)))
            args.append(tab)
    outs = pl.pallas_call(
        functools.partial(_inproj_body, kinds=kinds, has_qk=has_qk, has_rope=has_rope),
        grid=(m // tm,),
        in_specs=in_specs,
        out_specs=[pl.BlockSpec((tm, REGION), lambda i: (i, 0)) for _ in kinds],
        out_shape=[jax.ShapeDtypeStruct((m, REGION), BF16) for _ in kinds],
        compiler_params=_params(1),
        name="inproj",
    )(*args)
    return outs


def _inproj_t_body(h_ref, wt_ref, o_ref):
    h = h_ref[...]
    for c0 in range(0, REGION, PROJ_CHUNK):
        acc = lax.dot_general(wt_ref[c0:c0 + PROJ_CHUNK, :], h, (((1,), (1,)), ((), ())),
                              preferred_element_type=F32)
        o_ref[c0:c0 + PROJ_CHUNK, :] = acc.astype(BF16)


def _inproj_t(h, w_t, layer, tm):
    m, d = h.shape
    return pl.pallas_call(
        _inproj_t_body,
        grid=(m // tm,),
        in_specs=[pl.BlockSpec((tm, d), lambda i: (i, 0)),
                  pl.BlockSpec((None, REGION, d), lambda i: (layer, 0, 0))],
        out_specs=pl.BlockSpec((REGION, tm), lambda i: (0, i)),
        out_shape=jax.ShapeDtypeStruct((REGION, m), BF16),
        compiler_params=_params(1),
        name="inproj_t",
    )(h, w_t)


def _attn_body(*refs, lam_init, n_lat_chunks, tk, has_lat):
    lq1, lk1, lq2, lk2, sw_ref, q_ref = refs[:6]
    if has_lat:
        k_ref, vt_ref, kc_ref, vct_ref, o_ref = refs[6:]
    else:
        kc_ref, vct_ref, o_ref = refs[6:]
    q = q_ref[...]
    tq = q.shape[0]
    lane = lax.broadcasted_iota(jnp.int32, q.shape, 1)
    zero = jnp.zeros_like(q)
    qs = jnp.concatenate([jnp.where(lane < DIFF_HEAD_DIM, q, zero),
                          jnp.where(lane >= DIFF_HEAD_DIM, q, zero)], axis=0)

    def step(k, vt, carry):
        m, l, acc = carry
        s = lax.dot_general(k, qs, (((1,), (1,)), ((), ())), preferred_element_type=F32)
        m_new = jnp.maximum(m, jnp.max(s, axis=0, keepdims=True))
        alpha = jnp.exp2(m - m_new)
        p = jnp.exp2(s - m_new)
        l = alpha * l + jnp.sum(p, axis=0, keepdims=True)
        acc = alpha * acc + jnp.dot(vt, p.astype(BF16), preferred_element_type=F32)
        return m_new, l, acc

    carry = (jnp.full((1, 2 * tq), -jnp.inf, F32),
             jnp.zeros((1, 2 * tq), F32),
             jnp.zeros((V_HEAD_DIM, 2 * tq), F32))
    if has_lat:
        for c in range(n_lat_chunks):
            carry = step(k_ref[c * tk:(c + 1) * tk, :], vt_ref[:, c * tk:(c + 1) * tk], carry)
    carry = step(kc_ref[...], vct_ref[...], carry)
    _, l, acc = carry
    o = acc * (1.0 / l)
    lam = (jnp.exp(jnp.sum(lq1[...] * lk1[...], axis=-1, keepdims=True))
           - jnp.exp(jnp.sum(lq2[...] * lk2[...], axis=-1, keepdims=True)) + lam_init)
    od = o[:, :tq] - lam * o[:, tq:]
    yt = od * lax.rsqrt(jnp.mean(od * od, axis=0, keepdims=True) + EPS)
    o_ref[...] = (yt.T * (sw_ref[...] * (1.0 - lam_init))).astype(BF16)


def _attention(q, k, vt, kc, vct, lams, subln_w, lam_init, nb, sq, sk, sc, tq, tk):
    n_heads = q.shape[1] // V_HEAD_DIM
    nq = sq // tq
    has_lat = k is not None
    small = pl.BlockSpec((1, DIFF_HEAD_DIM), lambda b, h, i: (0, 0))
    in_specs = [small, small, small, small,
                pl.BlockSpec((1, V_HEAD_DIM), lambda b, h, i: (0, 0)),
                pl.BlockSpec((tq, V_HEAD_DIM), lambda b, h, i: (b * nq + i, h))]
    args = [a.astype(F32).reshape(1, DIFF_HEAD_DIM) for a in lams]
    args += [subln_w.astype(F32).reshape(1, V_HEAD_DIM), q]
    if has_lat:
        in_specs += [pl.BlockSpec((sk, V_HEAD_DIM), lambda b, h, i: (b, h)),
                     pl.BlockSpec((V_HEAD_DIM, sk), lambda b, h, i: (h, b))]
        args += [k, vt]
    in_specs += [pl.BlockSpec((sc, V_HEAD_DIM), lambda b, h, i: (b, h)),
                 pl.BlockSpec((V_HEAD_DIM, sc), lambda b, h, i: (h, b))]
    args += [kc, vct]
    return pl.pallas_call(
        functools.partial(_attn_body, lam_init=lam_init,
                          n_lat_chunks=(sk // tk if has_lat else 0), tk=tk, has_lat=has_lat),
        grid=(nb, n_heads, nq),
        in_specs=in_specs,
        out_specs=pl.BlockSpec((tq, V_HEAD_DIM), lambda b, h, i: (b * nq + i, h)),
        out_shape=jax.ShapeDtypeStruct(q.shape, BF16),
        compiler_params=_params(3),
        name="diff_attention",
    )(*args)


def _stack_maps(q):
    lane = lax.broadcasted_iota(jnp.int32, q.shape, 1)
    zero = jnp.zeros_like(q)
    return jnp.concatenate([jnp.where(lane < DIFF_HEAD_DIM, q, zero),
                            jnp.where(lane >= DIFF_HEAD_DIM, q, zero)], axis=0)


def _attn_lat_body(lq1, lk1, lq2, lk2, sw_ref, q_ref, k_ref, vt_ref, kc_ref, vct_ref, o_ref,
                   s0_scr, s1_scr, m0_scr, m1_scr, *, lam_init, tq, tk):
    sk, sc = k_ref.shape[0], kc_ref.shape[0]
    nq = q_ref.shape[0] // tq
    assert nq % 2 == 0
    dn = (((1,), (1,)), ((), ()))
    bufs = ((s0_scr, m0_scr), (s1_scr, m1_scr))

    def tile_rows(i):
        return pl.ds(i * tq if isinstance(i, int) else pl.multiple_of(i * tq, tq), tq)

    def scores(i, buf):
        s_scr, m_scr = buf
        qs = _stack_maps(q_ref[tile_rows(i), :])
        mx = None
        for c0 in range(0, sk, tk):
            s = lax.dot_general(k_ref[c0:c0 + tk, :], qs, dn, preferred_element_type=F32)
            s_scr[c0:c0 + tk, :] = s
            cm = jnp.max(s, axis=0, keepdims=True)
            mx = cm if mx is None else jnp.maximum(mx, cm)
        s = lax.dot_general(kc_ref[...], qs, dn, preferred_element_type=F32)
        s_scr[sk:sk + sc, :] = s
        m_scr[...] = jnp.maximum(mx, jnp.max(s, axis=0, keepdims=True))

    lam = (jnp.exp(jnp.sum(lq1[...] * lk1[...], axis=-1, keepdims=True))
           - jnp.exp(jnp.sum(lq2[...] * lk2[...], axis=-1, keepdims=True)) + lam_init)
    out_w = sw_ref[...] * (1.0 - lam_init)

    def weigh(i, buf):
        s_scr, m_scr = buf
        m = m_scr[...]
        acc = None
        for c0 in range(0, sk + sc, tk):
            c1 = min(c0 + tk, sk + sc)
            p = jnp.exp2(s_scr[c0:c1, :] - m).astype(BF16)
            vt = vt_ref[:, c0:c1] if c0 < sk else vct_ref[...]
            vt1 = jnp.concatenate([vt, jnp.ones((BF16_SUBLANES, c1 - c0), BF16)], axis=0)
            pv = jnp.dot(vt1, p, preferred_element_type=F32)
            acc = pv if acc is None else acc + pv
        l = acc[V_HEAD_DIM:V_HEAD_DIM + 1, :]
        o = acc[:V_HEAD_DIM, :] * (1.0 / l)
        od = o[:, :tq] - lam * o[:, tq:]
        yt = od * lax.rsqrt(jnp.mean(od * od, axis=0, keepdims=True) + EPS)
        o_ref[tile_rows(i), :] = (yt.T * out_w).astype(BF16)

    scores(0, bufs[0])

    def body(j, carry):
        i = 2 * j
        scores(i + 1, bufs[1])
        weigh(i, bufs[0])
        scores(i + 2, bufs[0])
        weigh(i + 1, bufs[1])
        return carry

    lax.fori_loop(0, nq // 2 - 1, body, 0)
    scores(nq - 1, bufs[1])
    weigh(nq - 2, bufs[0])
    weigh(nq - 1, bufs[1])


def _attention_lat(q, k, vt, kc, vct, lams, subln_w, lam_init, nb, sq, sc, tq, tk):
    n_heads = q.shape[1] // V_HEAD_DIM
    small = pl.BlockSpec((1, DIFF_HEAD_DIM), lambda b, h: (0, 0))
    rows = lambda b, h: (b, h)
    cols = lambda b, h: (h, b)
    in_specs = [small, small, small, small,
                pl.BlockSpec((1, V_HEAD_DIM), lambda b, h: (0, 0)),
                pl.BlockSpec((sq, V_HEAD_DIM), rows),
                pl.BlockSpec((sq, V_HEAD_DIM), rows),
                pl.BlockSpec((V_HEAD_DIM, sq), cols),
                pl.BlockSpec((sc, V_HEAD_DIM), rows),
                pl.BlockSpec((V_HEAD_DIM, sc), cols)]
    args = [a.astype(F32).reshape(1, DIFF_HEAD_DIM) for a in lams]
    args += [subln_w.astype(F32).reshape(1, V_HEAD_DIM), q, k, vt, kc, vct]
    return pl.pallas_call(
        functools.partial(_attn_lat_body, lam_init=lam_init, tq=tq, tk=tk),
        grid=(nb, n_heads),
        in_specs=in_specs,
        out_specs=pl.BlockSpec((sq, V_HEAD_DIM), rows),
        out_shape=jax.ShapeDtypeStruct(q.shape, BF16),
        scratch_shapes=[pltpu.VMEM((sq + sc, 2 * tq), F32), pltpu.VMEM((sq + sc, 2 * tq), F32),
                        pltpu.VMEM((1, 2 * tq), F32), pltpu.VMEM((1, 2 * tq), F32)],
        compiler_params=_params(2),
        name="diff_attention_lat",
    )(*args)


def _band_matrices(tm):
    t = np.arange(tm)[:, None]
    j = np.arange(tm + 2 * POOL_HALO)[None, :] - POOL_HALO
    mats = [((j >= t - w // 2) & (j < t + w // 2)).astype(np.float32) for w in POOL_WINDOWS]
    return jnp.asarray(np.stack(mats), dtype=BF16)


def _merge_body(pv_ref, prev_ref, next_ref, pg_ref, o_ref, ag_ref, x_ref, gate_ref,
                band_ref, pw_ref, ps_ref, wo_ref, out_ref, *, tiles_per_seq, seq_len):
    i = pl.program_id(0)
    tm = pv_ref.shape[0]
    ti = i % tiles_per_seq
    main = pv_ref[...]
    zero_halo = jnp.zeros_like(prev_ref[...])
    prev = jnp.where(ti == 0, zero_halo, prev_ref[...])
    nxt = jnp.where(ti == tiles_per_seq - 1, zero_halo, next_ref[...])
    ext = jnp.concatenate([prev, main, nxt], axis=0)
    tpos = ti * tm + lax.broadcasted_iota(jnp.int32, (tm, 1), 0)
    gw = main.shape[1] // N_POOL_GROUPS
    d_pool = main.shape[1]
    y = None
    for g, w in enumerate(POOL_WINDOWS):
        sl = slice(g * gw, (g + 1) * gw)
        wsum = jnp.dot(band_ref[g], ext[:, sl], preferred_element_type=F32)
        cnt = (jnp.minimum(tpos + w // 2, seq_len) - jnp.maximum(tpos - w // 2, 0)).astype(F32)
        pooled = wsum / cnt - main[:, sl].astype(F32)
        mixed = jnp.dot(pooled.astype(BF16), pw_ref[g], preferred_element_type=F32)
        pool_o = (mixed * ps_ref[:, sl] * pg_ref[:, sl].astype(F32)).astype(BF16)
        part = jnp.dot(pool_o, wo_ref[g * gw:(g + 1) * gw, :], preferred_element_type=F32)
        y = part if y is None else y + part
    attn_o = (o_ref[...].astype(F32) * ag_ref[...].astype(F32)).astype(BF16)
    y = y + jnp.dot(attn_o, wo_ref[d_pool:, :], preferred_element_type=F32)
    out_ref[...] = x_ref[...] + gate_ref[...] * y


def _merge(pool_v, pool_g, o, attn_g, x2, mod4, layer, cond_row, pool_w_bf, pool_scale, w_out_bf,
           seq_len, tm):
    m, d_pool = pool_v.shape
    d = x2.shape[1]
    tiles_per_seq = seq_len // tm
    hb = tm // POOL_HALO
    n_halo_blocks = m // POOL_HALO
    row = lambda i: (i, 0)
    in_specs = [
        pl.BlockSpec((tm, d_pool), row),
        pl.BlockSpec((POOL_HALO, d_pool), lambda i: (jnp.maximum(i * hb - 1, 0), 0)),
        pl.BlockSpec((POOL_HALO, d_pool), lambda i: (jnp.minimum((i + 1) * hb, n_halo_blocks - 1), 0)),
        pl.BlockSpec((tm, d_pool), row),
        pl.BlockSpec((tm, d_pool), row),
        pl.BlockSpec((tm, d_pool), row),
        pl.BlockSpec((tm, d), row),
        pl.BlockSpec((None, None, 1, d), lambda i: (layer, cond_row(i // tiles_per_seq), 0, 2)),
        pl.BlockSpec((N_POOL_GROUPS, tm, tm + 2 * POOL_HALO), lambda i: (0, 0, 0)),
        pl.BlockSpec((None, N_POOL_GROUPS, d_pool // N_POOL_GROUPS, d_pool // N_POOL_GROUPS),
                     lambda i: (layer, 0, 0, 0)),
        pl.BlockSpec((None, 1, d_pool), lambda i: (layer, 0, 0)),
        pl.BlockSpec((None, 2 * d_pool, d), lambda i: (layer, 0, 0)),
    ]
    return pl.pallas_call(
        functools.partial(_merge_body, tiles_per_seq=tiles_per_seq, seq_len=seq_len),
        grid=(m // tm,),
        in_specs=in_specs,
        out_specs=pl.BlockSpec((tm, d), row),
        out_shape=jax.ShapeDtypeStruct((m, d), F32),
        compiler_params=_params(1),
        name="merge",
    )(pool_v, pool_v, pool_v, pool_g, o, attn_g, x2, mod4, _band_matrices(tm), pool_w_bf,
      pool_scale.reshape(pool_scale.shape[0], 1, d_pool), w_out_bf)


def _rope_tables(seq_len):
    rows = seq_len // GRID_W
    row = jnp.broadcast_to(jnp.arange(rows)[:, None], (rows, GRID_W)).reshape(-1).astype(F32)
    col = jnp.broadcast_to(jnp.arange(GRID_W)[None, :], (rows, GRID_W)).reshape(-1).astype(F32)
    inv_freq = ROPE_BASE ** (-jnp.arange(ROPE_PAIRS, dtype=F32) / ROPE_PAIRS)
    ang_r = row[:, None] * inv_freq
    ang_c = col[:, None] * inv_freq
    cos64 = jnp.concatenate([jnp.cos(ang_r), jnp.cos(ang_r), jnp.cos(ang_c), jnp.cos(ang_c)], axis=-1)
    sin64 = jnp.concatenate([-jnp.sin(ang_r), jnp.sin(ang_r), -jnp.sin(ang_c), jnp.sin(ang_c)], axis=-1)
    return jnp.tile(cos64, (1, 2)), jnp.tile(sin64, (1, 2))


def kernel(x, c, ctx, c_ctx, norm_w, w_ada, b_ada, w_in, pool_w, pool_scale, q_norm_w, k_norm_w,
           lambda_q1, lambda_k1, lambda_q2, lambda_k2, subln_w, w_out):
    nb, seq, d = x.shape
    n_ctx = ctx.shape[1]
    depth = w_in.shape[0]
    d_pool = pool_scale.shape[1]
    o_pool_v, o_q, o_v = 0, 2 * d_pool, 2 * d_pool + 2 * REGION
    o_k = o_q + REGION

    cond = jnp.concatenate([c, c_ctx[None, :]], axis=0)
    cond = jnp.pad(cond, ((0, COND_ROWS - cond.shape[0]), (0, 0)))
    mod = _adaln(cond, w_ada, b_ada)
    mod4 = mod.reshape(depth, COND_ROWS, 1, 3 * d)
    w_in_bf = w_in.astype(BF16)
    w_out_bf = w_out.astype(BF16)
    pool_w_bf = pool_w.astype(BF16)
    w_vt = jnp.swapaxes(w_in_bf[:, :, o_v:o_v + REGION], 1, 2)
    o_attn_g = o_v + REGION
    rope_tabs = _rope_tables(seq)
    q_scale = DIFF_HEAD_DIM ** -0.5 * math.log2(math.e)

    lat_row = lambda b: b
    ctx_row = lambda b: nb
    tm_lat, tm_ctx = 1024, n_ctx
    tq, tk = 256, 512

    x2 = x.reshape(nb * seq, d)
    ctx2 = ctx.reshape(nb * n_ctx, d)
    for l in range(depth):
        update_ctx = l < depth - 1
        lam_init = 0.8 - 0.6 * math.exp(-0.3 * l)
        lams = (lambda_q1[l], lambda_k1[l], lambda_q2[l], lambda_k2[l])
        qn, kn = q_norm_w[l], k_norm_w[l]

        h = _norm_modulate(x2.reshape(nb, seq, d), norm_w, mod4, l, lat_row, 512)
        hc = _norm_modulate(ctx2.reshape(nb, n_ctx, d), norm_w, mod4, l, ctx_row, n_ctx)

        lat = functools.partial(_inproj, h, w_in_bf, l, tm=tm_lat, tiles_per_seq=seq // tm_lat)
        pool_v, pool_g = lat(o_pool_v, (("plain",), ("silu",)), None, None)
        q, k = lat(o_q, (("qk", True, q_scale), ("qk", True, 1.0)), (qn, kn), rope_tabs)
        vt = _inproj_t(h, w_vt, l, tm_lat)
        (attn_g,) = lat(o_attn_g, (("silu",),), None, None)

        cx = functools.partial(_inproj, hc, w_in_bf, l, tm=tm_ctx, tiles_per_seq=1)
        vct = _inproj_t(hc, w_vt, l, tm_ctx)
        if update_ctx:
            pool_vc, pool_gc = cx(o_pool_v, (("plain",), ("silu",)), None, None)
            qc, kc = cx(o_q, (("qk", False, q_scale), ("qk", False, 1.0)), (qn, kn), None)
            (attn_gc,) = cx(o_attn_g, (("silu",),), None, None)
        else:
            (kc,) = cx(o_k, (("qk", False, 1.0),), (kn,), None)

        o = _attention_lat(q, k, vt, kc, vct, lams, subln_w[l], lam_init, nb, seq, n_ctx, tq, tk)
        x2 = _merge(pool_v, pool_g, o, attn_g, x2, mod4, l, lat_row, pool_w_bf, pool_scale,
                    w_out_bf, seq, 512)
        if update_ctx:
            oc = _attention(qc, None, None, kc, vct, lams, subln_w[l], lam_init,
                            nb, n_ctx, 0, n_ctx, n_ctx, tk)
            ctx2 = _merge(pool_vc, pool_gc, oc, attn_gc, ctx2, mod4, l, ctx_row, pool_w_bf,
                          pool_scale, w_out_bf, n_ctx, n_ctx)
    return x2.reshape(nb, seq, d)
```

```python
import functools
import math

import numpy as np
import jax
import jax.numpy as jnp
from jax import lax
from jax.experimental import pallas as pl
from jax.experimental.pallas import tpu as pltpu

F32 = jnp.float32
BF16 = jnp.bfloat16

GRID_W = 64
POOL_WINDOWS = (2, 4, 8, 16)
N_POOL_GROUPS = len(POOL_WINDOWS)
DIFF_HEAD_DIM = 64
V_HEAD_DIM = 2 * DIFF_HEAD_DIM
ROPE_PAIRS = DIFF_HEAD_DIM // 4
ROPE_BASE = 10000.0
EPS = 1e-6

LANES = 128
BF16_SUBLANES = 16
MXU_DIM = 256
VMEM_LIMIT_BYTES = 56 * 1024 * 1024
COND_ROWS = 16
POOL_HALO = BF16_SUBLANES


def _params(n_axes):
    return pltpu.CompilerParams(
        dimension_semantics=("arbitrary",) * n_axes,
        vmem_limit_bytes=VMEM_LIMIT_BYTES)


def _silu(a):
    return a * jax.nn.sigmoid(a)


def _adaln_body(cond_ref, w_ref, b_ref, o_ref):
    s = _silu(cond_ref[...]).astype(BF16)
    o_ref[...] = jnp.dot(s, w_ref[...].astype(BF16), preferred_element_type=F32) + b_ref[...]


def _adaln(cond, w_ada, b_ada):
    depth, d, n = w_ada.shape
    tn = 512
    return pl.pallas_call(
        _adaln_body,
        grid=(depth, n // tn),
        in_specs=[
            pl.BlockSpec((COND_ROWS, d), lambda l, j: (0, 0)),
            pl.BlockSpec((None, d, tn), lambda l, j: (l, 0, j)),
            pl.BlockSpec((None, 1, tn), lambda l, j: (l, 0, j)),
        ],
        out_specs=pl.BlockSpec((None, COND_ROWS, tn), lambda l, j: (l, 0, j)),
        out_shape=jax.ShapeDtypeStruct((depth, COND_ROWS, n), F32),
        compiler_params=_params(2),
        name="adaln",
    )(cond, w_ada, b_ada.reshape(depth, 1, n))


def _norm_body(x_ref, nw_ref, shift_ref, scale_ref, o_ref):
    x = x_ref[...]
    y = x * lax.rsqrt(jnp.mean(x * x, axis=-1, keepdims=True) + EPS) * nw_ref[...]
    o_ref[...] = (y * (1.0 + scale_ref[...]) + shift_ref[...]).astype(BF16)


def _norm_modulate(x, norm_w, mod4, layer, cond_row, tm):
    nb, s, d = x.shape
    nt = s // tm
    return pl.pallas_call(
        _norm_body,
        grid=(nb, nt),
        in_specs=[
            pl.BlockSpec((None, tm, d), lambda b, i: (b, i, 0)),
            pl.BlockSpec((None, 1, d), lambda b, i: (layer, 0, 0)),
            pl.BlockSpec((None, None, 1, d), lambda b, i: (layer, cond_row(b), 0, 0)),
            pl.BlockSpec((None, None, 1, d), lambda b, i: (layer, cond_row(b), 0, 1)),
        ],
        out_specs=pl.BlockSpec((tm, d), lambda b, i: (b * nt + i, 0)),
        out_shape=jax.ShapeDtypeStruct((nb * s, d), BF16),
        compiler_params=_params(2),
        name="norm_modulate",
    )(x, norm_w.reshape(norm_w.shape[0], 1, d), mod4, mod4)


REGION = 1024
PROJ_CHUNK = 512


def _group_mean_matrix():
    g = np.arange(MXU_DIM) // DIFF_HEAD_DIM
    return jnp.asarray((g[:, None] == g[None, :]).astype(np.float32) / DIFF_HEAD_DIM, dtype=BF16)


def _qk_epilogue(acc, gm, nw, cos, sin, out_scale):
    outs = []
    for s0 in range(0, PROJ_CHUNK, MXU_DIM):
        a = acc[:, s0:s0 + MXU_DIM]
        ms = jnp.dot((a * a).astype(BF16), gm, preferred_element_type=F32)
        y = a * lax.rsqrt(ms + EPS)
        for h0 in range(0, MXU_DIM, LANES):
            yh = y[:, h0:h0 + LANES] * nw
            if cos is not None:
                lane = lax.broadcasted_iota(jnp.int32, yh.shape, 1)
                first = (lane & (2 * ROPE_PAIRS - 1)) < ROPE_PAIRS
                partner = jnp.where(first,
                                    pltpu.roll(yh, LANES - ROPE_PAIRS, axis=1),
                                    pltpu.roll(yh, ROPE_PAIRS, axis=1))
                yh = yh * cos + partner * sin
            if out_scale != 1.0:
                yh = yh * out_scale
            outs.append(yh.astype(BF16))
    return outs


def _inproj_body(*refs, kinds, has_qk, has_rope):
    h_ref, w_ref = refs[0], refs[1]
    pos = 2
    gm = nws = cos = sin = None
    if has_qk:
        gm = refs[pos][...]
        pos += 1
        nws = []
        for kind in kinds:
            if kind[0] == "qk":
                nws.append(refs[pos][...])
                pos += 1
            else:
                nws.append(None)
    if has_rope:
        cos = refs[pos][...]
        sin = refs[pos + 1][...]
        pos += 2
    out_refs = refs[pos:]
    h = h_ref[...]
    for r, kind in enumerate(kinds):
        for c0 in range(0, REGION, PROJ_CHUNK):
            col = r * REGION + c0
            acc = jnp.dot(h, w_ref[:, col:col + PROJ_CHUNK], preferred_element_type=F32)
            if kind[0] == "plain":
                out_refs[r][:, c0:c0 + PROJ_CHUNK] = acc.astype(BF16)
            elif kind[0] == "silu":
                out_refs[r][:, c0:c0 + PROJ_CHUNK] = _silu(acc).astype(BF16)
            else:
                _, rope, out_scale = kind
                pieces = _qk_epilogue(acc, gm, nws[r],
                                      cos if rope else None, sin if rope else None, out_scale)
                for i, piece in enumerate(pieces):
                    out_refs[r][:, c0 + i * LANES:c0 + (i + 1) * LANES] = piece


def _inproj(h, w_in_bf, layer, col_start, kinds, norm_ws, rope_tabs, tm, tiles_per_seq):
    m, d = h.shape
    ncols = REGION * len(kinds)
    has_qk = any(k[0] == "qk" for k in kinds)
    has_rope = any(k[0] == "qk" and k[1] for k in kinds)
    in_specs = [
        pl.BlockSpec((tm, d), lambda i: (i, 0)),
        pl.BlockSpec((None, d, ncols), lambda i: (layer, 0, col_start // ncols)),
    ]
    args = [h, w_in_bf]
    if has_qk:
        in_specs.append(pl.BlockSpec((MXU_DIM, MXU_DIM), lambda i: (0, 0)))
        args.append(_group_mean_matrix())
        for nw in norm_ws:
            if nw is not None:
                in_specs.append(pl.BlockSpec((1, LANES), lambda i: (0, 0)))
                args.append(jnp.tile(nw.astype(F32), LANES // DIFF_HEAD_DIM).reshape(1, LANES))
    if has_rope:
        for tab in rope_tabs:
            in_specs.append(pl.BlockSpec((tm, LANES), lambda i: (i % tiles_per_seq, 0)))
            args.append(tab)
    outs = pl.pallas_call(
        functools.partial(_inproj_body, kinds=kinds, has_qk=has_qk, has_rope=has_rope),
        grid=(m // tm,),
        in_specs=in_specs,
        out_specs=[pl.BlockSpec((tm, REGION), lambda i: (i, 0)) for _ in kinds],
        out_shape=[jax.ShapeDtypeStruct((m, REGION), BF16) for _ in kinds],
        compiler_params=_params(1),
        name="inproj",
    )(*args)
    return outs


def _inproj_t_body(h_ref, wt_ref, o_ref):
    h = h_ref[...]
    for c0 in range(0, REGION, PROJ_CHUNK):
        acc = lax.dot_general(wt_ref[c0:c0 + PROJ_CHUNK, :], h, (((1,), (1,)), ((), ())),
                              preferred_element_type=F32)
        o_ref[c0:c0 + PROJ_CHUNK, :] = acc.astype(BF16)


def _inproj_t(h, w_t, layer, tm):
    m, d = h.shape
    return pl.pallas_call(
        _inproj_t_body,
        grid=(m // tm,),
        in_specs=[pl.BlockSpec((tm, d), lambda i: (i, 0)),
                  pl.BlockSpec((None, REGION, d), lambda i: (layer, 0, 0))],
        out_specs=pl.BlockSpec((REGION, tm), lambda i: (0, i)),
        out_shape=jax.ShapeDtypeStruct((REGION, m), BF16),
        compiler_params=_params(1),
        name="inproj_t",
    )(h, w_t)


def _attn_body(*refs, lam_init, n_lat_chunks, tk, has_lat):
    lq1, lk1, lq2, lk2, sw_ref, q_ref = refs[:6]
    if has_lat:
        k_ref, vt_ref, kc_ref, vct_ref, o_ref = refs[6:]
    else:
        kc_ref, vct_ref, o_ref = refs[6:]
    q = q_ref[...]
    tq = q.shape[0]
    lane = lax.broadcasted_iota(jnp.int32, q.shape, 1)
    zero = jnp.zeros_like(q)
    qs = jnp.concatenate([jnp.where(lane < DIFF_HEAD_DIM, q, zero),
                          jnp.where(lane >= DIFF_HEAD_DIM, q, zero)], axis=0)

    def step(k, vt, carry):
        m, l, acc = carry
        s = lax.dot_general(k, qs, (((1,), (1,)), ((), ())), preferred_element_type=F32)
        m_new = jnp.maximum(m, jnp.max(s, axis=0, keepdims=True))
        alpha = jnp.exp2(m - m_new)
        p = jnp.exp2(s - m_new)
        l = alpha * l + jnp.sum(p, axis=0, keepdims=True)
        acc = alpha * acc + jnp.dot(vt, p.astype(BF16), preferred_element_type=F32)
        return m_new, l, acc

    carry = (jnp.full((1, 2 * tq), -jnp.inf, F32),
             jnp.zeros((1, 2 * tq), F32),
             jnp.zeros((V_HEAD_DIM, 2 * tq), F32))
    if has_lat:
        for c in range(n_lat_chunks):
            carry = step(k_ref[c * tk:(c + 1) * tk, :], vt_ref[:, c * tk:(c + 1) * tk], carry)
    carry = step(kc_ref[...], vct_ref[...], carry)
    _, l, acc = carry
    o = acc * (1.0 / l)
    lam = (jnp.exp(jnp.sum(lq1[...] * lk1[...], axis=-1, keepdims=True))
           - jnp.exp(jnp.sum(lq2[...] * lk2[...], axis=-1, keepdims=True)) + lam_init)
    od = o[:, :tq] - lam * o[:, tq:]
    yt = od * lax.rsqrt(jnp.mean(od * od, axis=0, keepdims=True) + EPS)
    o_ref[...] = (yt.T * (sw_ref[...] * (1.0 - lam_init))).astype(BF16)


def _attention(q, k, vt, kc, vct, lams, subln_w, lam_init, nb, sq, sk, sc, tq, tk):
    n_heads = q.shape[1] // V_HEAD_DIM
    nq = sq // tq
    has_lat = k is not None
    small = pl.BlockSpec((1, DIFF_HEAD_DIM), lambda b, h, i: (0, 0))
    in_specs = [small, small, small, small,
                pl.BlockSpec((1, V_HEAD_DIM), lambda b, h, i: (0, 0)),
                pl.BlockSpec((tq, V_HEAD_DIM), lambda b, h, i: (b * nq + i, h))]
    args = [a.astype(F32).reshape(1, DIFF_HEAD_DIM) for a in lams]
    args += [subln_w.astype(F32).reshape(1, V_HEAD_DIM), q]
    if has_lat:
        in_specs += [pl.BlockSpec((sk, V_HEAD_DIM), lambda b, h, i: (b, h)),
                     pl.BlockSpec((V_HEAD_DIM, sk), lambda b, h, i: (h, b))]
        args += [k, vt]
    in_specs += [pl.BlockSpec((sc, V_HEAD_DIM), lambda b, h, i: (b, h)),
                 pl.BlockSpec((V_HEAD_DIM, sc), lambda b, h, i: (h, b))]
    args += [kc, vct]
    return pl.pallas_call(
        functools.partial(_attn_body, lam_init=lam_init,
                          n_lat_chunks=(sk // tk if has_lat else 0), tk=tk, has_lat=has_lat),
        grid=(nb, n_heads, nq),
        in_specs=in_specs,
        out_specs=pl.BlockSpec((tq, V_HEAD_DIM), lambda b, h, i: (b * nq + i, h)),
        out_shape=jax.ShapeDtypeStruct(q.shape, BF16),
        compiler_params=_params(3),
        name="diff_attention",
    )(*args)


def _stack_maps(q):
    lane = lax.broadcasted_iota(jnp.int32, q.shape, 1)
    zero = jnp.zeros_like(q)
    return jnp.concatenate([jnp.where(lane < DIFF_HEAD_DIM, q, zero),
                            jnp.where(lane >= DIFF_HEAD_DIM, q, zero)], axis=0)


def _attn_lat_body(lq1, lk1, lq2, lk2, sw_ref, q_ref, k_ref, vt_ref, kc_ref, vct_ref, o_ref,
                   s0_scr, s1_scr, m0_scr, m1_scr, *, lam_init, tq, tk):
    sk, sc = k_ref.shape[0], kc_ref.shape[0]
    nq = q_ref.shape[0] // tq
    assert nq % 2 == 0
    dn = (((1,), (1,)), ((), ()))
    bufs = ((s0_scr, m0_scr), (s1_scr, m1_scr))

    def tile_rows(i):
        return pl.ds(i * tq if isinstance(i, int) else pl.multiple_of(i * tq, tq), tq)

    def scores(i, buf):
        s_scr, m_scr = buf
        qs = _stack_maps(q_ref[tile_rows(i), :])
        mx = None
        for c0 in range(0, sk, tk):
            s = lax.dot_general(k_ref[c0:c0 + tk, :], qs, dn, preferred_element_type=F32)
            s_scr[c0:c0 + tk, :] = s
            cm = jnp.max(s, axis=0, keepdims=True)
            mx = cm if mx is None else jnp.maximum(mx, cm)
        s = lax.dot_general(kc_ref[...], qs, dn, preferred_element_type=F32)
        s_scr[sk:sk + sc, :] = s
        m_scr[...] = jnp.maximum(mx, jnp.max(s, axis=0, keepdims=True))

    lam = (jnp.exp(jnp.sum(lq1[...] * lk1[...], axis=-1, keepdims=True))
           - jnp.exp(jnp.sum(lq2[...] * lk2[...], axis=-1, keepdims=True)) + lam_init)
    out_w = sw_ref[...] * (1.0 - lam_init)

    def weigh(i, buf):
        s_scr, m_scr = buf
        m = m_scr[...]
        acc = None
        l = None
        for c0 in range(0, sk + sc, tk):
            c1 = min(c0 + tk, sk + sc)
            p = jnp.exp2(s_scr[c0:c1, :] - m)
            ps = jnp.sum(p, axis=0, keepdims=True)
            vt = vt_ref[:, c0:c1] if c0 < sk else vct_ref[...]
            pv = jnp.dot(vt, p.astype(BF16), preferred_element_type=F32)
            acc = pv if acc is None else acc + pv
            l = ps if l is None else l + ps
        o = acc * (1.0 / l)
        od = o[:, :tq] - lam * o[:, tq:]
        yt = od * lax.rsqrt(jnp.mean(od * od, axis=0, keepdims=True) + EPS)
        o_ref[tile_rows(i), :] = (yt.T * out_w).astype(BF16)

    scores(0, bufs[0])

    def body(j, carry):
        i = 2 * j
        scores(i + 1, bufs[1])
        weigh(i, bufs[0])
        scores(i + 2, bufs[0])
        weigh(i + 1, bufs[1])
        return carry

    lax.fori_loop(0, nq // 2 - 1, body, 0)
    scores(nq - 1, bufs[1])
    weigh(nq - 2, bufs[0])
    weigh(nq - 1, bufs[1])


def _attention_lat(q, k, vt, kc, vct, lams, subln_w, lam_init, nb, sq, sc, tq, tk):
    n_heads = q.shape[1] // V_HEAD_DIM
    small = pl.BlockSpec((1, DIFF_HEAD_DIM), lambda b, h: (0, 0))
    rows = lambda b, h: (b, h)
    cols = lambda b, h: (h, b)
    in_specs = [small, small, small, small,
                pl.BlockSpec((1, V_HEAD_DIM), lambda b, h: (0, 0)),
                pl.BlockSpec((sq, V_HEAD_DIM), rows),
                pl.BlockSpec((sq, V_HEAD_DIM), rows),
                pl.BlockSpec((V_HEAD_DIM, sq), cols),
                pl.BlockSpec((sc, V_HEAD_DIM), rows),
                pl.BlockSpec((V_HEAD_DIM, sc), cols)]
    args = [a.astype(F32).reshape(1, DIFF_HEAD_DIM) for a in lams]
    args += [subln_w.astype(F32).reshape(1, V_HEAD_DIM), q, k, vt, kc, vct]
    return pl.pallas_call(
        functools.partial(_attn_lat_body, lam_init=lam_init, tq=tq, tk=tk),
        grid=(nb, n_heads),
        in_specs=in_specs,
        out_specs=pl.BlockSpec((sq, V_HEAD_DIM), rows),
        out_shape=jax.ShapeDtypeStruct(q.shape, BF16),
        scratch_shapes=[pltpu.VMEM((sq + sc, 2 * tq), F32), pltpu.VMEM((sq + sc, 2 * tq), F32),
                        pltpu.VMEM((1, 2 * tq), F32), pltpu.VMEM((1, 2 * tq), F32)],
        compiler_params=_params(2),
        name="diff_attention_lat",
    )(*args)


def _window_sum(xe, w):
    n = xe.shape[0]
    acc = xe + pltpu.roll(xe, 1, axis=0)
    half = 1
    while 2 * half < w:
        acc = pltpu.roll(acc, half, axis=0) + pltpu.roll(acc, n - half, axis=0)
        half *= 2
    return acc


def _merge_body(*refs, tiles_per_seq, seq_len, emit_next):
    (pv_ref, prev_ref, next_ref, pg_ref, o_ref, ag_ref, x_ref, gate_ref,
     pw_ref, ps_ref, wo_ref) = refs[:11]
    if emit_next:
        nw_ref, shift_ref, scale_ref, out_ref, hn_ref = refs[11:]
    else:
        (out_ref,) = refs[11:]
    i = pl.program_id(0)
    tm = pv_ref.shape[0]
    ti = i % tiles_per_seq
    main = pv_ref[...].astype(F32)
    zero_halo = jnp.zeros((POOL_HALO, main.shape[1]), F32)
    prev = jnp.where(ti == 0, zero_halo, prev_ref[...].astype(F32))
    nxt = jnp.where(ti == tiles_per_seq - 1, zero_halo, next_ref[...].astype(F32))
    ext = jnp.concatenate([prev, main, nxt], axis=0)
    tpos = ti * tm + lax.broadcasted_iota(jnp.int32, (tm, 1), 0)
    gw = main.shape[1] // N_POOL_GROUPS
    d_pool = main.shape[1]
    y = None
    for g, w in enumerate(POOL_WINDOWS):
        sl = slice(g * gw, (g + 1) * gw)
        wsum = _window_sum(ext[:, sl], w)[POOL_HALO:POOL_HALO + tm]
        cnt = (jnp.minimum(tpos + w // 2, seq_len) - jnp.maximum(tpos - w // 2, 0)).astype(F32)
        pooled = wsum / cnt - main[:, sl]
        mixed = jnp.dot(pooled.astype(BF16), pw_ref[g], preferred_element_type=F32)
        pool_o = (mixed * ps_ref[:, sl] * pg_ref[:, sl].astype(F32)).astype(BF16)
        part = jnp.dot(pool_o, wo_ref[g * gw:(g + 1) * gw, :], preferred_element_type=F32)
        y = part if y is None else y + part
    attn_o = (o_ref[...].astype(F32) * ag_ref[...].astype(F32)).astype(BF16)
    y = y + jnp.dot(attn_o, wo_ref[d_pool:, :], preferred_element_type=F32)
    out = x_ref[...] + gate_ref[...] * y
    out_ref[...] = out
    if emit_next:
        hn = out * lax.rsqrt(jnp.mean(out * out, axis=-1, keepdims=True) + EPS) * nw_ref[...]
        hn_ref[...] = (hn * (1.0 + scale_ref[...]) + shift_ref[...]).astype(BF16)


def _merge(pool_v, pool_g, o, attn_g, x2, mod4, layer, cond_row, pool_w_bf, pool_scale, w_out_bf,
           seq_len, tm, norm_w=None):
    m, d_pool = pool_v.shape
    d = x2.shape[1]
    emit_next = norm_w is not None
    tiles_per_seq = seq_len // tm
    hb = tm // POOL_HALO
    n_halo_blocks = m // POOL_HALO
    row = lambda i: (i, 0)
    mod_spec = lambda lyr, blk: pl.BlockSpec(
        (None, None, 1, d), lambda i: (lyr, cond_row(i // tiles_per_seq), 0, blk))
    in_specs = [
        pl.BlockSpec((tm, d_pool), row),
        pl.BlockSpec((POOL_HALO, d_pool), lambda i: (jnp.maximum(i * hb - 1, 0), 0)),
        pl.BlockSpec((POOL_HALO, d_pool), lambda i: (jnp.minimum((i + 1) * hb, n_halo_blocks - 1), 0)),
        pl.BlockSpec((tm, d_pool), row),
        pl.BlockSpec((tm, d_pool), row),
        pl.BlockSpec((tm, d_pool), row),
        pl.BlockSpec((tm, d), row),
        mod_spec(layer, 2),
        pl.BlockSpec((None, N_POOL_GROUPS, d_pool // N_POOL_GROUPS, d_pool // N_POOL_GROUPS),
                     lambda i: (layer, 0, 0, 0)),
        pl.BlockSpec((None, 1, d_pool), lambda i: (layer, 0, 0)),
        pl.BlockSpec((None, 2 * d_pool, d), lambda i: (layer, 0, 0)),
    ]
    args = [pool_v, pool_v, pool_v, pool_g, o, attn_g, x2, mod4, pool_w_bf,
            pool_scale.reshape(pool_scale.shape[0], 1, d_pool), w_out_bf]
    out_specs = [pl.BlockSpec((tm, d), row)]
    out_shape = [jax.ShapeDtypeStruct((m, d), F32)]
    if emit_next:
        in_specs += [pl.BlockSpec((None, 1, d), lambda i: (layer + 1, 0, 0)),
                     mod_spec(layer + 1, 0), mod_spec(layer + 1, 1)]
        args += [norm_w.reshape(norm_w.shape[0], 1, d), mod4, mod4]
        out_specs.append(pl.BlockSpec((tm, d), row))
        out_shape.append(jax.ShapeDtypeStruct((m, d), BF16))
    outs = pl.pallas_call(
        functools.partial(_merge_body, tiles_per_seq=tiles_per_seq, seq_len=seq_len,
                          emit_next=emit_next),
        grid=(m // tm,),
        in_specs=in_specs,
        out_specs=out_specs,
        out_shape=out_shape,
        compiler_params=_params(1),
        name="merge",
    )(*args)
    return outs if emit_next else outs[0]


def _rope_tables(seq_len):
    rows = seq_len // GRID_W
    row = jnp.broadcast_to(jnp.arange(rows)[:, None], (rows, GRID_W)).reshape(-1).astype(F32)
    col = jnp.broadcast_to(jnp.arange(GRID_W)[None, :], (rows, GRID_W)).reshape(-1).astype(F32)
    inv_freq = ROPE_BASE ** (-jnp.arange(ROPE_PAIRS, dtype=F32) / ROPE_PAIRS)
    ang_r = row[:, None] * inv_freq
    ang_c = col[:, None] * inv_freq
    cos64 = jnp.concatenate([jnp.cos(ang_r), jnp.cos(ang_r), jnp.cos(ang_c), jnp.cos(ang_c)], axis=-1)
    sin64 = jnp.concatenate([-jnp.sin(ang_r), jnp.sin(ang_r), -jnp.sin(ang_c), jnp.sin(ang_c)], axis=-1)
    return jnp.tile(cos64, (1, 2)), jnp.tile(sin64, (1, 2))


def kernel(x, c, ctx, c_ctx, norm_w, w_ada, b_ada, w_in, pool_w, pool_scale, q_norm_w, k_norm_w,
           lambda_q1, lambda_k1, lambda_q2, lambda_k2, subln_w, w_out):
    nb, seq, d = x.shape
    n_ctx = ctx.shape[1]
    depth = w_in.shape[0]
    d_pool = pool_scale.shape[1]
    o_pool_v, o_q, o_v = 0, 2 * d_pool, 2 * d_pool + 2 * REGION
    o_k = o_q + REGION

    cond = jnp.concatenate([c, c_ctx[None, :]], axis=0)
    cond = jnp.pad(cond, ((0, COND_ROWS - cond.shape[0]), (0, 0)))
    mod = _adaln(cond, w_ada, b_ada)
    mod4 = mod.reshape(depth, COND_ROWS, 1, 3 * d)
    w_in_bf = w_in.astype(BF16)
    w_out_bf = w_out.astype(BF16)
    pool_w_bf = pool_w.astype(BF16)
    w_vt = jnp.swapaxes(w_in_bf[:, :, o_v:o_v + REGION], 1, 2)
    o_attn_g = o_v + REGION
    rope_tabs = _rope_tables(seq)
    q_scale = DIFF_HEAD_DIM ** -0.5 * math.log2(math.e)

    lat_row = lambda b: b
    ctx_row = lambda b: nb
    tm_lat, tm_ctx = 1024, n_ctx
    tq, tk = 256, 512

    x2 = x.reshape(nb * seq, d)
    ctx2 = ctx.reshape(nb * n_ctx, d)
    h = _norm_modulate(x, norm_w, mod4, 0, lat_row, 512)
    hc = _norm_modulate(ctx, norm_w, mod4, 0, ctx_row, n_ctx)
    for l in range(depth):
        update_ctx = l < depth - 1
        next_norm = norm_w if update_ctx else None
        lam_init = 0.8 - 0.6 * math.exp(-0.3 * l)
        lams = (lambda_q1[l], lambda_k1[l], lambda_q2[l], lambda_k2[l])
        qn, kn = q_norm_w[l], k_norm_w[l]

        lat = functools.partial(_inproj, h, w_in_bf, l, tm=tm_lat, tiles_per_seq=seq // tm_lat)
        pool_v, pool_g = lat(o_pool_v, (("plain",), ("silu",)), None, None)
        q, k = lat(o_q, (("qk", True, q_scale), ("qk", True, 1.0)), (qn, kn), rope_tabs)
        vt = _inproj_t(h, w_vt, l, tm_lat)
        (attn_g,) = lat(o_attn_g, (("silu",),), None, None)

        cx = functools.partial(_inproj, hc, w_in_bf, l, tm=tm_ctx, tiles_per_seq=1)
        vct = _inproj_t(hc, w_vt, l, tm_ctx)
        if update_ctx:
            pool_vc, pool_gc = cx(o_pool_v, (("plain",), ("silu",)), None, None)
            qc, kc = cx(o_q, (("qk", False, q_scale), ("qk", False, 1.0)), (qn, kn), None)
            (attn_gc,) = cx(o_attn_g, (("silu",),), None, None)
        else:
            (kc,) = cx(o_k, (("qk", False, 1.0),), (kn,), None)

        o = _attention_lat(q, k, vt, kc, vct, lams, subln_w[l], lam_init, nb, seq, n_ctx, tq, tk)
        merged = _merge(pool_v, pool_g, o, attn_g, x2, mod4, l, lat_row, pool_w_bf, pool_scale,
                        w_out_bf, seq, 512, next_norm)
        if update_ctx:
            x2, h = merged
            oc = _attention(qc, None, None, kc, vct, lams, subln_w[l], lam_init,
                            nb, n_ctx, 0, n_ctx, n_ctx, tk)
            ctx2, hc = _merge(pool_vc, pool_gc, oc, attn_gc, ctx2, mod4, l, ctx_row, pool_w_bf,
                              pool_scale, w_out_bf, n_ctx, n_ctx, next_norm)
        else:
            x2 = merged
    return x2.reshape(nb, seq, d)
```
